```python
import math
import jax
import jax.numpy as jnp
from jax import lax
import numpy as np

D_MODEL = 4096
BATCH = 2
SEQ = 8192
DEPTH = 4

GRID_W = 64
N_MIXERS = 3
D_FF = 4 * D_MODEL
NORM_EPS = 1e-6
SUBLN_EPS = 1e-5
NEG_INF = -1e30
ATTN_BLOCK = 128

A_GROUPS = ((128, 1), (512, 4), (2048, 16))
A_HEADS = 16
A_QK_DIM = 128
A_V_DIM = D_MODEL // A_HEADS
A_OUT = A_HEADS * A_V_DIM
A_QK_COLS = len(A_GROUPS) * A_HEADS * A_QK_DIM
A_IN = 2 * A_QK_COLS + A_OUT

B_HEAD_DIM = 128
B_HEADS = D_MODEL // (2 * B_HEAD_DIM)
B_V_DIM = 2 * B_HEAD_DIM
B_IN = 3 * D_MODEL

C_HEADS = 32
C_HEAD_DIM = D_MODEL // C_HEADS
C_IN = 3 * D_MODEL
NA_KH = 8
NA_KW = 16
NA_RB = 2
NA_CB = NA_KW

N_A_LAYERS = (DEPTH + N_MIXERS - 1) // N_MIXERS
N_B_LAYERS = (DEPTH + N_MIXERS - 2) // N_MIXERS
N_C_LAYERS = (DEPTH + N_MIXERS - 3) // N_MIXERS

kernel_name = 'hybrid_dilated_diff_natten_encoder'


def _alibi_slopes(n_heads):
    return jnp.exp2(jnp.arange(1, n_heads + 1, dtype=jnp.float32) * (-8.0 / n_heads))


def _rms_norm(x, g):
    xf = x.astype(jnp.float32)
    y = xf * lax.rsqrt(jnp.mean(xf * xf, axis=-1, keepdims=True) + NORM_EPS)
    return (y * g.astype(jnp.float32)).astype(x.dtype)


def _lambda_init(layer):
    return 0.8 - 0.6 * math.exp(-0.3 * layer)


def _dilated_window_stats(q, k, v, window, dil, slopes):
    B, S, H, _ = q.shape
    dv = v.shape[-1]
    half = window // (2 * dil)
    blk = half
    L = S // dil
    nb = -(-L // blk)
    Lp = nb * blk

    def strided(t, left, right):
        c = t.shape[-1]
        t = t.reshape(B, L, dil, H, c)
        t = jnp.pad(t, ((0, 0), (left, right), (0, 0), (0, 0), (0, 0)))
        return t.reshape(B, -1, blk, dil, H, c)

    qs = strided(q, 0, Lp - L)
    ks = strided(k, blk, Lp - L + blk)
    vs = strided(v, blk, Lp - L + blk)
    s = jnp.concatenate(
        [jnp.einsum('bnqrhd,bnkrhd->bnrhqk', qs, ks[:, j:j + nb]) for j in range(3)],
        axis=-1)
    off = np.arange(3 * blk)[None, :] - blk - np.arange(blk)[:, None]
    kpos = (np.arange(nb) * blk)[:, None, None] + np.arange(blk)[None, :, None] + off[None]
    valid = (np.abs(off) <= half)[None] & (kpos >= 0) & (kpos < L)
    dist = jnp.asarray(np.abs(off) * dil, dtype=jnp.float32)
    s = s - slopes[:, None, None] * dist
    s = jnp.where(jnp.asarray(valid)[None, :, None, None], s, NEG_INF)
    m = jnp.max(s, axis=-1)
    p = jnp.exp(s - m[..., None])
    l = jnp.sum(p, axis=-1)
    num = jnp.einsum('bnrhqk,bnkrhe->bnqrhe', p[..., :blk], vs[:, 0:nb])
    for j in range(1, 3):
        num = num + jnp.einsum('bnrhqk,bnkrhe->bnqrhe', p[..., j * blk:(j + 1) * blk], vs[:, j:j + nb])
    num = num.reshape(B, Lp, dil, H, dv)[:, :L].reshape(B, S, H, dv)

    def to_seq(t):
        return jnp.transpose(t, (0, 1, 4, 2, 3)).reshape(B, Lp, dil, H)[:, :L].reshape(B, S, H)

    return num, to_seq(m), to_seq(l)


def _dilated_mixer(h, w_in, w_out):
    B, S, _ = h.shape
    G = len(A_GROUPS)
    qkv = (h @ w_in).astype(jnp.float32)
    q = qkv[..., :A_QK_COLS].reshape(B, S, G, A_HEADS, A_QK_DIM) * (A_QK_DIM ** -0.5)
    k = qkv[..., A_QK_COLS:2 * A_QK_COLS].reshape(B, S, G, A_HEADS, A_QK_DIM)
    v = qkv[..., 2 * A_QK_COLS:].reshape(B, S, A_HEADS, A_V_DIM)
    slopes = _alibi_slopes(A_HEADS)
    stats = [_dilated_window_stats(q[:, :, g], k[:, :, g], v, win, dil, slopes)
             for g, (win, dil) in enumerate(A_GROUPS)]
    m_max = jnp.max(jnp.stack([st[1] for st in stats]), axis=0)
    num = jnp.zeros((B, S, A_HEADS, A_V_DIM), jnp.float32)
    den = jnp.zeros((B, S, A_HEADS), jnp.float32)
    for st in stats:
        w = jnp.exp(st[1] - m_max)
        num = num + w[..., None] * st[0]
        den = den + w * st[2]
    o = num / den[..., None]
    return o.reshape(B, S, A_OUT).astype(h.dtype) @ w_out


def _diff_mixer(h, w_in, w_out, lq1, lk1, lq2, lk2, subln_g, lambda_init):
    B, S, _ = h.shape
    qkv = (h @ w_in).astype(jnp.float32)
    q, k, v = jnp.split(qkv, 3, axis=-1)
    q = q.reshape(B, S, B_HEADS, 2, B_HEAD_DIM) * (B_HEAD_DIM ** -0.5)
    k = k.reshape(B, S, B_HEADS, 2, B_HEAD_DIM)
    v = v.reshape(B, S, B_HEADS, B_V_DIM)
    f32 = jnp.float32
    lam = (jnp.exp(jnp.sum(lq1.astype(f32) * lk1.astype(f32)))
           - jnp.exp(jnp.sum(lq2.astype(f32) * lk2.astype(f32))) + lambda_init)
    slopes = _alibi_slopes(B_HEADS)
    kpos = jnp.arange(S)

    def block(bi):
        t0 = bi * ATTN_BLOCK
        qb = lax.dynamic_slice_in_dim(q, t0, ATTN_BLOCK, axis=1)
        s = jnp.einsum('bqhcd,bkhcd->bhcqk', qb, k)
        qpos = t0 + jnp.arange(ATTN_BLOCK)
        dist = jnp.abs(qpos[:, None] - kpos[None, :]).astype(f32)
        s = s - slopes[None, :, None, None, None] * dist
        p = jax.nn.softmax(s, axis=-1)
        a = p[:, :, 0] - lam * p[:, :, 1]
        return jnp.einsum('bhqk,bkhe->bqhe', a, v)

    o = lax.map(block, jnp.arange(S // ATTN_BLOCK))
    o = jnp.moveaxis(o, 0, 1).reshape(B, S, B_HEADS, B_V_DIM)
    o = o * lax.rsqrt(jnp.mean(o * o, axis=-1, keepdims=True) + SUBLN_EPS)
    o = o * subln_g.astype(f32) * (1.0 - lambda_init)
    return o.reshape(B, S, D_MODEL).astype(h.dtype) @ w_out


def _na_mixer(h, w_in, w_out, rpb):
    B, S, _ = h.shape
    rows = S // GRID_W
    kh = min(NA_KH, rows)
    rr = min(NA_RB + kh - 1, rows)
    n_cb = GRID_W // NA_CB
    cr = NA_CB + NA_KW - 1
    qkv = (h @ w_in).astype(jnp.float32)
    q, k, v = jnp.split(qkv, 3, axis=-1)
    q = q.reshape(B, rows, GRID_W, C_HEADS, C_HEAD_DIM) * (C_HEAD_DIM ** -0.5)
    k = k.reshape(B, rows, GRID_W, C_HEADS, C_HEAD_DIM)
    v = v.reshape(B, rows, GRID_W, C_HEADS, C_HEAD_DIM)
    q_col = np.arange(GRID_W).reshape(n_cb, NA_CB)
    win_c = np.clip(q_col - NA_KW // 2, 0, GRID_W - NA_KW)
    reg_c = np.clip(np.arange(n_cb) * NA_CB - NA_KW // 2, 0, GRID_W - cr)
    key_col = reg_c[:, None] + np.arange(cr)
    col_ok = jnp.asarray((key_col[:, None, :] >= win_c[:, :, None])
                         & (key_col[:, None, :] < win_c[:, :, None] + NA_KW))
    dc_idx = np.clip(key_col[:, None, :] - q_col[:, :, None] + NA_KW - 1, 0, 2 * NA_KW - 2)
    rpb = rpb.astype(jnp.float32)

    def row_block(rb):
        r0 = rb * NA_RB
        q_row = r0 + jnp.arange(NA_RB)
        win_r = jnp.clip(q_row - kh // 2, 0, rows - kh)
        reg_r = jnp.clip(win_r[0], 0, rows - rr)
        key_row = reg_r + jnp.arange(rr)
        row_ok = (key_row[None, :] >= win_r[:, None]) & (key_row[None, :] < win_r[:, None] + kh)
        dr_idx = jnp.clip(key_row[None, :] - q_row[:, None] + NA_KH - 1, 0, 2 * NA_KH - 2)
        qb = lax.dynamic_slice_in_dim(q, r0, NA_RB, axis=1).reshape(
            B, NA_RB, n_cb, NA_CB, C_HEADS, C_HEAD_DIM)
        kb = lax.dynamic_slice_in_dim(k, reg_r, rr, axis=1)[:, :, key_col]
        vb = lax.dynamic_slice_in_dim(v, reg_r, rr, axis=1)[:, :, key_col]
        s = jnp.einsum('binjhd,brnmhd->bhnijrm', qb, kb)
        bias = rpb[:, dr_idx[None, :, None, :, None], dc_idx[:, None, :, None, :]]
        mask = row_ok[None, :, None, :, None] & col_ok[:, None, :, None, :]
        s = jnp.where(mask, s + bias, NEG_INF)
        shp = s.shape
        p = jax.nn.softmax(s.reshape(shp[:-2] + (rr * cr,)), axis=-1).reshape(shp)
        o = jnp.einsum('bhnijrm,brnmhd->binjhd', p, vb)
        return o.reshape(B, NA_RB, GRID_W, C_HEADS, C_HEAD_DIM)

    o = lax.map(row_block, jnp.arange(rows // NA_RB))
    o = jnp.moveaxis(o, 0, 1).reshape(B, S, D_MODEL)
    return o.astype(h.dtype) @ w_out


def _sq_relu_mlp(h, w_up, w_down):
    u = jax.nn.relu(h @ w_up)
    return (u * u) @ w_down


def setup_inputs(seed: int = 0) -> dict:
    key = jax.random.key(seed)
    ks = jax.random.split(key, 20)
    f32 = jnp.float32

    def nrm(k, shape, scale):
        return jax.random.normal(k, shape, f32) * scale

    return {
        'x': nrm(ks[0], (BATCH, SEQ, D_MODEL), 1.0),
        'mix_norm_g': 1.0 + nrm(ks[1], (DEPTH, D_MODEL), 0.02),
        'mlp_norm_g': 1.0 + nrm(ks[2], (DEPTH, D_MODEL), 0.02),
        'final_norm_g': 1.0 + nrm(ks[3], (D_MODEL,), 0.02),
        'a_w_in': nrm(ks[4], (N_A_LAYERS, D_MODEL, A_IN), D_MODEL ** -0.5),
        'a_w_out': nrm(ks[5], (N_A_LAYERS, A_OUT, D_MODEL), A_OUT ** -0.5),
        'b_w_in': nrm(ks[6], (N_B_LAYERS, D_MODEL, B_IN), D_MODEL ** -0.5),
        'b_w_out': nrm(ks[7], (N_B_LAYERS, D_MODEL, D_MODEL), D_MODEL ** -0.5),
        'b_lambda_q1': nrm(ks[8], (N_B_LAYERS, B_HEAD_DIM), 0.1),
        'b_lambda_k1': nrm(ks[9], (N_B_LAYERS, B_HEAD_DIM), 0.1),
        'b_lambda_q2': nrm(ks[10], (N_B_LAYERS, B_HEAD_DIM), 0.1),
        'b_lambda_k2': nrm(ks[11], (N_B_LAYERS, B_HEAD_DIM), 0.1),
        'b_subln_g': 1.0 + nrm(ks[12], (N_B_LAYERS, B_V_DIM), 0.02),
        'c_w_in': nrm(ks[13], (N_C_LAYERS, D_MODEL, C_IN), D_MODEL ** -0.5),
        'c_w_out': nrm(ks[14], (N_C_LAYERS, D_MODEL, D_MODEL), D_MODEL ** -0.5),
        'c_rpb': nrm(ks[15], (N_C_LAYERS, C_HEADS, 2 * NA_KH - 1, 2 * NA_KW - 1), 0.1),
        'mlp_w_up': nrm(ks[16], (DEPTH, D_MODEL, D_FF), D_MODEL ** -0.5),
        'mlp_w_down': nrm(ks[17], (DEPTH, D_FF, D_MODEL), D_FF ** -0.5),
    }


def reference(x, mix_norm_g, mlp_norm_g, final_norm_g, a_w_in, a_w_out, b_w_in, b_w_out,
              b_lambda_q1, b_lambda_k1, b_lambda_q2, b_lambda_k2, b_subln_g,
              c_w_in, c_w_out, c_rpb, mlp_w_up, mlp_w_down):
    for i in range(DEPTH):
        kind = i % N_MIXERS
        j = i // N_MIXERS
        h = _rms_norm(x, mix_norm_g[i])
        if kind == 0:
            y = _dilated_mixer(h, a_w_in[j], a_w_out[j])
        elif kind == 1:
            y = _diff_mixer(h, b_w_in[j], b_w_out[j], b_lambda_q1[j], b_lambda_k1[j],
                            b_lambda_q2[j], b_lambda_k2[j], b_subln_g[j], _lambda_init(i))
        else:
            y = _na_mixer(h, c_w_in[j], c_w_out[j], c_rpb[j])
        x = x + y
        x = x + _sq_relu_mlp(_rms_norm(x, mlp_norm_g[i]), mlp_w_up[i], mlp_w_down[i])
    return _rms_norm(x, final_norm_g)
```

```python
import functools
import math

import numpy as np
import jax
import jax.numpy as jnp
from jax import lax
from jax.experimental import pallas as pl
from jax.experimental.pallas import tpu as pltpu

F32 = jnp.float32
BF16 = jnp.bfloat16

V7X_VMEM_BYTES = 64 * 2**20
VMEM_LIMIT_BYTES = V7X_VMEM_BYTES - 8 * 2**20
LANES = 128

D_MODEL = 4096
NORM_EPS = 1e-6
SUBLN_EPS = 1e-5
NEG_INF = -1e30
K_SLICE = 4096

A_GROUPS = ((128, 1), (512, 4), (2048, 16))
A_HEADS = 16
A_QK_DIM = 128
A_V_DIM = 256
A_QK_COLS = len(A_GROUPS) * A_HEADS * A_QK_DIM
A_IN = 2 * A_QK_COLS + A_HEADS * A_V_DIM
A_HALF = 64

B_HEADS = 16
B_HEAD_DIM = 128
B_V_DIM = 256

C_HEADS = 32
C_HEAD_DIM = 128
GRID_W = 64
NA_KH = 8
NA_KW = 16
NA_QROWS = 4
NA_KROWS = 3 * NA_QROWS


def _alibi_slope(h):
    return float(2.0 ** (-8.0 * (h + 1) / 16.0))


def _params(semantics, vmem_bytes=VMEM_LIMIT_BYTES):
    return pltpu.CompilerParams(dimension_semantics=semantics, vmem_limit_bytes=vmem_bytes)


def _rmsnorm_kernel(x_ref, g_ref, o_ref):
    x = x_ref[...]
    ms = jnp.mean(x * x, axis=-1, keepdims=True)
    o_ref[...] = (x * lax.rsqrt(ms + NORM_EPS) * g_ref[...]).astype(o_ref.dtype)


def _rmsnorm(x2d, g, out_dtype, tm=512):
    m, d = x2d.shape
    return pl.pallas_call(
        _rmsnorm_kernel,
        grid=(m // tm,),
        in_specs=[pl.BlockSpec((tm, d), lambda i: (i, 0)),
                  pl.BlockSpec((1, d), lambda i: (0, 0))],
        out_specs=pl.BlockSpec((tm, d), lambda i: (i, 0)),
        out_shape=jax.ShapeDtypeStruct((m, d), out_dtype),
        compiler_params=_params(("arbitrary",)),
        name="rmsnorm",
    )(x2d, g.reshape(1, d))


def _matmul_kernel(*refs, epilogue):
    if epilogue == "residual":
        x_ref, w_ref, r_ref, o_ref, wbf_ref = refs
    else:
        x_ref, w_ref, o_ref, wbf_ref = refs

    @pl.when(pl.program_id(1) == 0)
    def _():
        wbf_ref[...] = w_ref[...].astype(BF16)

    acc = jnp.dot(x_ref[...], wbf_ref[...], preferred_element_type=F32)
    if epilogue == "residual":
        o_ref[...] = r_ref[...] + acc
    elif epilogue == "relu2":
        r = jnp.maximum(acc, 0.0)
        o_ref[...] = (r * r).astype(o_ref.dtype)
    else:
        o_ref[...] = acc.astype(o_ref.dtype)


def _matmul(x, w_stack, layer, *, epilogue, k_block=0, res=None, tm=1024, tn=512):
    m = x.shape[0]
    n = w_stack.shape[2]
    grid = (n // tn, m // tm)
    in_specs = [pl.BlockSpec((tm, K_SLICE), lambda j, i: (i, k_block)),
                pl.BlockSpec((None, K_SLICE, tn), lambda j, i: (layer, k_block, j))]
    args = [x, w_stack]
    if epilogue == "residual":
        in_specs.append(pl.BlockSpec((tm, tn), lambda j, i: (i, j)))
        args.append(res)
        out_dtype = F32
    else:
        out_dtype = BF16
    return pl.pallas_call(
        functools.partial(_matmul_kernel, epilogue=epilogue),
        grid=grid,
        in_specs=in_specs,
        out_specs=pl.BlockSpec((tm, tn), lambda j, i: (i, j)),
        out_shape=jax.ShapeDtypeStruct((m, n), out_dtype),
        scratch_shapes=[pltpu.VMEM((K_SLICE, tn), BF16)],
        compiler_params=_params(("arbitrary", "arbitrary")),
        name="matmul_" + epilogue,
    )(*args)


def _dilated_kernel(q_ref, kp_ref, kc_ref, kn_ref, vp_ref, vc_ref, vn_ref,
                    num_ref, m_ref, l_ref, *, dil, length, tq):
    u0 = pl.program_id(2) * tq
    nk = tq + 2 * A_HALF
    row = lax.broadcasted_iota(jnp.int32, (tq, nk), 0)
    col = lax.broadcasted_iota(jnp.int32, (tq, nk), 1)
    off = col - A_HALF - row
    kpos = u0 + col - A_HALF
    valid = (jnp.abs(off) <= A_HALF) & (kpos >= 0) & (kpos < length)
    dist = jnp.abs(off).astype(F32) * float(dil)
    lane = lax.broadcasted_iota(jnp.int32, (tq, LANES), 1)
    m_all = jnp.zeros((tq, LANES), F32)
    l_all = jnp.ones((tq, LANES), F32)
    scale = A_QK_DIM ** -0.5
    for h in range(A_HEADS):
        qs = slice(h * A_QK_DIM, (h + 1) * A_QK_DIM)
        vs = slice(h * A_V_DIM, (h + 1) * A_V_DIM)
        q = q_ref[0, :, qs]
        k = jnp.concatenate([kp_ref[0, :, qs], kc_ref[0, :, qs], kn_ref[0, :, qs]], axis=0)
        s = lax.dot_general(q, k, (((1,), (1,)), ((), ())), preferred_element_type=F32) * scale
        s = s - _alibi_slope(h) * dist
        s = jnp.where(valid, s, NEG_INF)
        m = jnp.max(s, axis=-1, keepdims=True)
        p = jnp.exp(s - m)
        l = jnp.sum(p, axis=-1, keepdims=True)
        v = jnp.concatenate([vp_ref[0, :, vs], vc_ref[0, :, vs], vn_ref[0, :, vs]], axis=0)
        num_ref[0, :, vs] = jnp.dot(p.astype(BF16), v, preferred_element_type=F32)
        m_all = jnp.where(lane == h, m, m_all)
        l_all = jnp.where(lane == h, l, l_all)
    m_ref[0] = m_all
    l_ref[0] = l_all


def _dilated_group(qkv, g, dil, tq=128):
    b, s, _ = qkv.shape
    length = s // dil
    view = qkv.reshape(b, length, dil * A_IN)
    n_half = length // A_HALF
    per_half = tq // A_HALF
    qk_w = A_HEADS * A_QK_DIM
    v_w = A_HEADS * A_V_DIM
    q_col = lambda r: r * (A_IN // qk_w) + g
    k_col = lambda r: r * (A_IN // qk_w) + len(A_GROUPS) + g
    v_col = lambda r: r * (A_IN // v_w) + (2 * A_QK_COLS) // v_w
    prev = lambda u: jnp.maximum(u * per_half - 1, 0)
    nxt = lambda u: jnp.minimum((u + 1) * per_half, n_half - 1)
    in_specs = [
        pl.BlockSpec((1, tq, qk_w), lambda bi, r, u: (bi, u, q_col(r))),
        pl.BlockSpec((1, A_HALF, qk_w), lambda bi, r, u: (bi, prev(u), k_col(r))),
        pl.BlockSpec((1, tq, qk_w), lambda bi, r, u: (bi, u, k_col(r))),
        pl.BlockSpec((1, A_HALF, qk_w), lambda bi, r, u: (bi, nxt(u), k_col(r))),
        pl.BlockSpec((1, A_HALF, v_w), lambda bi, r, u: (bi, prev(u), v_col(r))),
        pl.BlockSpec((1, tq, v_w), lambda bi, r, u: (bi, u, v_col(r))),
        pl.BlockSpec((1, A_HALF, v_w), lambda bi, r, u: (bi, nxt(u), v_col(r))),
    ]
    out_specs = [
        pl.BlockSpec((1, tq, v_w), lambda bi, r, u: (bi, u, r)),
        pl.BlockSpec((1, tq, LANES), lambda bi, r, u: (bi, u, r)),
        pl.BlockSpec((1, tq, LANES), lambda bi, r, u: (bi, u, r)),
    ]
    out_shape = [
        jax.ShapeDtypeStruct((b, length, dil * v_w), F32),
        jax.ShapeDtypeStruct((b, length, dil * LANES), F32),
        jax.ShapeDtypeStruct((b, length, dil * LANES), F32),
    ]
    num, m, l = pl.pallas_call(
        functools.partial(_dilated_kernel, dil=dil, length=length, tq=tq),
        grid=(b, dil, length // tq),
        in_specs=in_specs,
        out_specs=out_specs,
        out_shape=out_shape,
        compiler_params=_params(("arbitrary", "arbitrary", "arbitrary")),
        name="dilated_attn_d%d" % dil,
    )(view, view, view, view, view, view, view)
    return num.reshape(b * s, v_w), m.reshape(b * s, LANES), l.reshape(b * s, LANES)


def _dilated_combine_kernel(n0_ref, n1_ref, n2_ref, m0_ref, m1_ref, m2_ref,
                            l0_ref, l1_ref, l2_ref, o_ref):
    m0, m1, m2 = m0_ref[...], m1_ref[...], m2_ref[...]
    m_max = jnp.maximum(jnp.maximum(m0, m1), m2)
    w0 = jnp.exp(m0 - m_max)
    w1 = jnp.exp(m1 - m_max)
    w2 = jnp.exp(m2 - m_max)
    den = w0 * l0_ref[...] + w1 * l1_ref[...] + w2 * l2_ref[...]
    for h in range(A_HEADS):
        vs = slice(h * A_V_DIM, (h + 1) * A_V_DIM)
        hs = slice(h, h + 1)
        num = w0[:, hs] * n0_ref[:, vs] + w1[:, hs] * n1_ref[:, vs] + w2[:, hs] * n2_ref[:, vs]
        o_ref[:, vs] = (num / den[:, hs]).astype(o_ref.dtype)


def _dilated_combine(stats, tm=256):
    (n0, m0, l0), (n1, m1, l1), (n2, m2, l2) = stats
    m, d = n0.shape
    big = pl.BlockSpec((tm, d), lambda i: (i, 0))
    small = pl.BlockSpec((tm, LANES), lambda i: (i, 0))
    return pl.pallas_call(
        _dilated_combine_kernel,
        grid=(m // tm,),
        in_specs=[big, big, big, small, small, small, small, small, small],
        out_specs=big,
        out_shape=jax.ShapeDtypeStruct((m, d), BF16),
        compiler_params=_params(("arbitrary",)),
        name="dilated_combine",
    )(n0, n1, n2, m0, m1, m2, l0, l1, l2)


def _dilated_attention(qkv):
    stats = [_dilated_group(qkv, g, dil) for g, (_, dil) in enumerate(A_GROUPS)]
    return _dilated_combine(stats)


def _diff_kernel(q_ref, k_ref, v_ref, lam_ref, g_ref, o_ref, m_sc, l_sc, acc_sc,
                 *, tq, tk, lambda_init):
    h = pl.program_id(1)
    qi = pl.program_id(2)
    ki = pl.program_id(3)

    @pl.when(ki == 0)
    def _():
        m_sc[...] = jnp.full(m_sc.shape, NEG_INF, F32)
        l_sc[...] = jnp.zeros(l_sc.shape, F32)
        acc_sc[...] = jnp.zeros(acc_sc.shape, F32)

    slope = jnp.exp2(jnp.full((1, 1), -8.0 / B_HEADS, F32) * (h + 1).astype(F32))
    row = lax.broadcasted_iota(jnp.int32, (tq, tk), 0)
    col = lax.broadcasted_iota(jnp.int32, (tq, tk), 1)
    dist = jnp.abs(row - col + (qi * tq - ki * tk)).astype(F32)
    bias = slope * dist
    scale = B_HEAD_DIM ** -0.5
    v = v_ref[0]
    for c in range(2):
        cs = slice(c * B_HEAD_DIM, (c + 1) * B_HEAD_DIM)
        s = lax.dot_general(q_ref[0, :, cs], k_ref[0, :, cs], (((1,), (1,)), ((), ())),
                            preferred_element_type=F32) * scale - bias
        m_prev = m_sc[c]
        m_new = jnp.maximum(m_prev, jnp.max(s, axis=-1, keepdims=True))
        alpha = jnp.exp(m_prev - m_new)
        p = jnp.exp(s - pltpu.repeat(m_new, tk // LANES, axis=1))
        l_sc[c] = alpha * l_sc[c] + jnp.sum(p, axis=-1, keepdims=True)
        acc_sc[c] = (pltpu.repeat(alpha, B_V_DIM // LANES, axis=1) * acc_sc[c]
                     + jnp.dot(p.astype(BF16), v, preferred_element_type=F32))
        m_sc[c] = m_new

    @pl.when(ki == pl.num_programs(3) - 1)
    def _():
        lam = (jnp.exp(jnp.sum(lam_ref[0:1, :] * lam_ref[1:2, :], axis=-1, keepdims=True))
               - jnp.exp(jnp.sum(lam_ref[2:3, :] * lam_ref[3:4, :], axis=-1, keepdims=True))
               + lambda_init)
        o0 = acc_sc[0] / pltpu.repeat(l_sc[0], B_V_DIM // LANES, axis=1)
        o1 = acc_sc[1] / pltpu.repeat(l_sc[1], B_V_DIM // LANES, axis=1)
        o = o0 - lam * o1
        o = o * lax.rsqrt(jnp.mean(o * o, axis=-1, keepdims=True) + SUBLN_EPS)
        o = o * g_ref[...] * (1.0 - lambda_init)
        o_ref[0] = o.astype(o_ref.dtype)


def _diff_attention(qkv, lam_rows, subln_g, lambda_init, tq=1024, tk=512):
    b, s, _ = qkv.shape
    w = 2 * B_HEAD_DIM
    out = pl.pallas_call(
        functools.partial(_diff_kernel, tq=tq, tk=tk, lambda_init=lambda_init),
        grid=(b, B_HEADS, s // tq, s // tk),
        in_specs=[
            pl.BlockSpec((1, tq, w), lambda bi, h, qi, ki: (bi, qi, h)),
            pl.BlockSpec((1, tk, w), lambda bi, h, qi, ki: (bi, ki, B_HEADS + h)),
            pl.BlockSpec((1, tk, w), lambda bi, h, qi, ki: (bi, ki, 2 * B_HEADS + h)),
            pl.BlockSpec((4, B_HEAD_DIM), lambda bi, h, qi, ki: (0, 0)),
            pl.BlockSpec((1, B_V_DIM), lambda bi, h, qi, ki: (0, 0)),
        ],
        out_specs=pl.BlockSpec((1, tq, w), lambda bi, h, qi, ki: (bi, qi, h)),
        out_shape=jax.ShapeDtypeStruct((b, s, D_MODEL), BF16),
        scratch_shapes=[pltpu.VMEM((2, tq, LANES), F32),
                        pltpu.VMEM((2, tq, LANES), F32),
                        pltpu.VMEM((2, tq, B_V_DIM), F32)],
        compiler_params=_params(("arbitrary", "arbitrary", "arbitrary", "arbitrary")),
        name="diff_attn",
    )(qkv, qkv, qkv, lam_rows, subln_g.reshape(1, B_V_DIM))
    return out.reshape(b * s, D_MODEL)


def _na_bias_table(rpb):
    qr = np.arange(NA_QROWS)
    kr = np.arange(NA_KROWS)
    col = np.arange(GRID_W)
    dr_idx = np.clip(kr[None, :] - NA_QROWS - qr[:, None] + NA_KH - 1, 0, 2 * NA_KH - 2)
    win_c = np.clip(col - NA_KW // 2, 0, GRID_W - NA_KW)
    col_ok = (col[None, :] >= win_c[:, None]) & (col[None, :] < win_c[:, None] + NA_KW)
    dc_idx = np.clip(col[None, :] - col[:, None] + NA_KW - 1, 0, 2 * NA_KW - 2)
    t = rpb.astype(F32)[:, dr_idx[:, None, :, None], dc_idx[None, :, None, :]]
    t = jnp.where(jnp.asarray(col_ok)[None, None, :, None, :], t, NEG_INF)
    return t.reshape(rpb.shape[0], NA_QROWS * GRID_W, NA_KROWS * GRID_W)


def _na_row_mask(rows):
    qr = np.arange(NA_QROWS)
    kr = np.arange(NA_KROWS)
    masks = []
    for r0 in (0, NA_QROWS, rows - NA_QROWS):
        win = np.clip(r0 + qr - NA_KH // 2, 0, rows - NA_KH)
        key_row = r0 - NA_QROWS + kr
        ok = (key_row[None, :] >= win[:, None]) & (key_row[None, :] < win[:, None] + NA_KH)
        ok = np.broadcast_to(ok[:, None, :, None], (NA_QROWS, GRID_W, NA_KROWS, GRID_W))
        masks.append(np.where(ok, 0.0, NEG_INF).reshape(NA_QROWS * GRID_W, NA_KROWS * GRID_W))
    return jnp.asarray(np.stack(masks), dtype=F32)


def _na_kernel(q_ref, kp_ref, kc_ref, kn_ref, vp_ref, vc_ref, vn_ref, bias_ref, mask_ref, o_ref, *, hb):
    scale = C_HEAD_DIM ** -0.5
    mask = mask_ref[0]
    for hh in range(hb):
        cs = slice(hh * C_HEAD_DIM, (hh + 1) * C_HEAD_DIM)
        k = jnp.concatenate([kp_ref[0, :, cs], kc_ref[0, :, cs], kn_ref[0, :, cs]], axis=0)
        v = jnp.concatenate([vp_ref[0, :, cs], vc_ref[0, :, cs], vn_ref[0, :, cs]], axis=0)
        s = lax.dot_general(q_ref[0, :, cs], k, (((1,), (1,)), ((), ())), preferred_element_type=F32)
        s = s * scale + bias_ref[hh] + mask
        m = jnp.max(s, axis=-1, keepdims=True)
        p = jnp.exp(s - m)
        l = jnp.sum(p, axis=-1, keepdims=True)
        o = jnp.dot(p.astype(BF16), v, preferred_element_type=F32) / l
        o_ref[0, :, cs] = o.astype(o_ref.dtype)


def _na_attention(qkv, rpb, hb=4):
    b, s, _ = qkv.shape
    rows = s // GRID_W
    tq = NA_QROWS * GRID_W
    nblk = s // tq
    w = hb * C_HEAD_DIM
    hgroups = C_HEADS // hb
    bias = _na_bias_table(rpb)
    mask = _na_row_mask(rows)
    prev = lambda r: jnp.maximum(r - 1, 0)
    nxt = lambda r: jnp.minimum(r + 1, nblk - 1)
    variant = lambda r: jnp.where(r == 0, 0, jnp.where(r == nblk - 1, 2, 1))
    kcol = lambda hg: hgroups + hg
    vcol = lambda hg: 2 * hgroups + hg
    out = pl.pallas_call(
        functools.partial(_na_kernel, hb=hb),
        grid=(b, hgroups, nblk),
        in_specs=[
            pl.BlockSpec((1, tq, w), lambda bi, hg, r: (bi, r, hg)),
            pl.BlockSpec((1, tq, w), lambda bi, hg, r: (bi, prev(r), kcol(hg))),
            pl.BlockSpec((1, tq, w), lambda bi, hg, r: (bi, r, kcol(hg))),
            pl.BlockSpec((1, tq, w), lambda bi, hg, r: (bi, nxt(r), kcol(hg))),
            pl.BlockSpec((1, tq, w), lambda bi, hg, r: (bi, prev(r), vcol(hg))),
            pl.BlockSpec((1, tq, w), lambda bi, hg, r: (bi, r, vcol(hg))),
            pl.BlockSpec((1, tq, w), lambda bi, hg, r: (bi, nxt(r), vcol(hg))),
            pl.BlockSpec((hb, tq, NA_KROWS * GRID_W), lambda bi, hg, r: (hg, 0, 0)),
            pl.BlockSpec((1, tq, NA_KROWS * GRID_W), lambda bi, hg, r: (variant(r), 0, 0)),
        ],
        out_specs=pl.BlockSpec((1, tq, w), lambda bi, hg, r: (bi, r, hg)),
        out_shape=jax.ShapeDtypeStruct((b, s, D_MODEL), BF16),
        compiler_params=_params(("arbitrary", "arbitrary", "arbitrary")),
        name="na_attn",
    )(qkv, qkv, qkv, qkv, qkv, qkv, qkv, bias, mask)
    return out.reshape(b * s, D_MODEL)


def _lambda_init(layer):
    return 0.8 - 0.6 * math.exp(-0.3 * layer)


def kernel(x, mix_norm_g, mlp_norm_g, final_norm_g, a_w_in, a_w_out, b_w_in, b_w_out,
           b_lambda_q1, b_lambda_k1, b_lambda_q2, b_lambda_k2, b_subln_g,
           c_w_in, c_w_out, c_rpb, mlp_w_up, mlp_w_down):
    b, s, d = x.shape
    depth = mix_norm_g.shape[0]
    xr = x.reshape(b * s, d)
    for i in range(depth):
        kind, j = i % 3, i // 3
        h = _rmsnorm(xr, mix_norm_g[i], BF16)
        if kind == 0:
            qkv = _matmul(h, a_w_in, j, epilogue="cast")
            o = _dilated_attention(qkv.reshape(b, s, A_IN))
            xr = _matmul(o, a_w_out, j, epilogue="residual", res=xr)
        elif kind == 1:
            qkv = _matmul(h, b_w_in, j, epilogue="cast")
            lam_rows = jnp.stack([b_lambda_q1[j], b_lambda_k1[j], b_lambda_q2[j], b_lambda_k2[j]]).astype(F32)
            o = _diff_attention(qkv.reshape(b, s, 3 * d), lam_rows, b_subln_g[j].astype(F32), _lambda_init(i))
            xr = _matmul(o, b_w_out, j, epilogue="residual", res=xr)
        else:
            qkv = _matmul(h, c_w_in, j, epilogue="cast")
            o = _na_attention(qkv.reshape(b, s, 3 * d), c_rpb[j])
            xr = _matmul(o, c_w_out, j, epilogue="residual", res=xr)
        h = _rmsnorm(xr, mlp_norm_g[i], BF16)
        u = _matmul(h, mlp_w_up, i, epilogue="relu2")
        for kb in range(mlp_w_up.shape[2] // K_SLICE):
            xr = _matmul(u, mlp_w_down, i, epilogue="residual", k_block=kb, res=xr)
    return _rmsnorm(xr, final_norm_g, F32).reshape(b, s, d)
```

```python
import functools
import math

import numpy as np
import jax
import jax.numpy as jnp
from jax import lax
from jax.experimental import pallas as pl
from jax.experimental.pallas import tpu as pltpu

F32 = jnp.float32
BF16 = jnp.bfloat16

V7X_VMEM_BYTES = 64 * 2**20
VMEM_LIMIT_BYTES = V7X_VMEM_BYTES - 8 * 2**20
LANES = 128

D_MODEL = 4096
NORM_EPS = 1e-6
SUBLN_EPS = 1e-5
NEG_INF = -1e30
LOG2E = math.log2(math.e)
K_SLICE = 4096

A_GROUPS = ((128, 1), (512, 4), (2048, 16))
A_HEADS = 16
A_QK_DIM = 128
A_V_DIM = 256
A_QK_COLS = len(A_GROUPS) * A_HEADS * A_QK_DIM
A_IN = 2 * A_QK_COLS + A_HEADS * A_V_DIM
A_HALF = 64
A_SUB = 2 * A_HALF
A_WIN = A_SUB + 2 * A_HALF
A_TB = A_SUB * max(d for _, d in A_GROUPS)
A_HALO = A_HALF * max(d for _, d in A_GROUPS)

B_HEADS = 16
B_HEAD_DIM = 128
B_V_DIM = 256
B_POS_SPLIT = 32
B_ROW_CHUNK = 32

C_HEADS = 32
C_HEAD_DIM = 128
GRID_W = 64
NA_KH = 8
NA_KW = 16
NA_QROWS = 4
NA_KROWS = 3 * NA_QROWS


def _params(semantics, vmem_bytes=VMEM_LIMIT_BYTES):
    return pltpu.CompilerParams(dimension_semantics=semantics, vmem_limit_bytes=vmem_bytes)


def _alibi_slope_vec(h, n_heads, shape):
    return jnp.exp2(jnp.full(shape, -8.0 / n_heads, F32) * (h + 1).astype(F32))


def _rmsnorm_kernel(x_ref, g_ref, o_ref):
    x = x_ref[...]
    ms = jnp.mean(x * x, axis=-1, keepdims=True)
    o_ref[...] = (x * lax.rsqrt(ms + NORM_EPS) * g_ref[...]).astype(o_ref.dtype)


def _rmsnorm(x2d, g, out_dtype, tm=512):
    m, d = x2d.shape
    return pl.pallas_call(
        _rmsnorm_kernel,
        grid=(m // tm,),
        in_specs=[pl.BlockSpec((tm, d), lambda i: (i, 0)),
                  pl.BlockSpec((1, d), lambda i: (0, 0))],
        out_specs=pl.BlockSpec((tm, d), lambda i: (i, 0)),
        out_shape=jax.ShapeDtypeStruct((m, d), out_dtype),
        compiler_params=_params(("arbitrary",)),
        name="rmsnorm",
    )(x2d, g.reshape(1, d))


def _matmul_kernel(*refs, epilogue, scaled_blocks, scale):
    if epilogue == "residual":
        x_ref, w_ref, r_ref, o_ref, wbf_ref = refs
    else:
        x_ref, w_ref, o_ref, wbf_ref = refs

    @pl.when(pl.program_id(1) == 0)
    def _():
        wbf_ref[...] = w_ref[...].astype(BF16)

    acc = jnp.dot(x_ref[...], wbf_ref[...], preferred_element_type=F32)
    if epilogue == "residual":
        o_ref[...] = r_ref[...] + acc
    elif epilogue == "relu2":
        r = jnp.maximum(acc, 0.0)
        o_ref[...] = (r * r).astype(o_ref.dtype)
    else:
        if scaled_blocks:
            acc = acc * jnp.where(pl.program_id(0) < scaled_blocks, scale, 1.0)
        o_ref[...] = acc.astype(o_ref.dtype)


def _matmul(x, w_stack, layer, *, epilogue, k_block=0, res=None, out_dtype=BF16,
            scaled_cols=0, scale=1.0, tm=1024, tn=512):
    m = x.shape[0]
    n = w_stack.shape[2]
    grid = (n // tn, m // tm)
    in_specs = [pl.BlockSpec((tm, K_SLICE), lambda j, i: (i, k_block)),
                pl.BlockSpec((None, K_SLICE, tn), lambda j, i: (layer, k_block, j))]
    args = [x, w_stack]
    if epilogue == "residual":
        in_specs.append(pl.BlockSpec((tm, tn), lambda j, i: (i, j)))
        args.append(res)
        out_dtype = F32
    assert scaled_cols % tn == 0
    return pl.pallas_call(
        functools.partial(_matmul_kernel, epilogue=epilogue, scaled_blocks=scaled_cols // tn, scale=scale),
        grid=grid,
        in_specs=in_specs,
        out_specs=pl.BlockSpec((tm, tn), lambda j, i: (i, j)),
        out_shape=jax.ShapeDtypeStruct((m, n), out_dtype),
        scratch_shapes=[pltpu.VMEM((K_SLICE, tn), BF16)],
        compiler_params=_params(("arbitrary", "arbitrary")),
        name="matmul_" + epilogue,
    )(*args)


def _dilated_kernel(q0_ref, q1_ref, q2_ref, k0_ref, k1_ref, k2_ref, va_ref, vb_ref, o_ref,
                    bias_sc, num_sc, m_sc, l_sc, *, seq):
    tb = pl.program_id(1)
    h = pl.program_id(2)
    t0 = tb * A_TB
    win_tokens = A_TB + 2 * A_HALO
    buf0 = jnp.clip(t0 - A_HALO, 0, seq - win_tokens)
    scale = A_QK_DIM ** -0.5

    slope = _alibi_slope_vec(h, A_HEADS, (1, 1))
    row = lax.broadcasted_iota(jnp.int32, (A_SUB, A_WIN), 0)
    col = lax.broadcasted_iota(jnp.int32, (A_SUB, A_WIN), 1)
    for g, (_, dil) in enumerate(A_GROUPS):
        for var, shift in enumerate((0, A_HALF, -A_HALF)):
            dist = jnp.abs(col + (shift - A_HALF) - row)
            tile = jnp.where(dist <= A_HALF, dist.astype(F32) * (slope * float(-dil)), NEG_INF)
            bias_sc[g * 3 + var] = tile

    q_refs = (q0_ref, q1_ref, q2_ref)
    k_refs = (k0_ref, k1_ref, k2_ref)
    for g, (_, dil) in enumerate(A_GROUPS):
        length = seq // dil
        q_ref, k_ref = q_refs[g], k_refs[g]

        def sub_block(i, carry, g=g, dil=dil, length=length, q_ref=q_ref, k_ref=k_ref):
            r = i % dil
            ub = i // dil
            qs = ub * (A_SUB * dil) + r
            w0 = t0 // dil + ub * A_SUB - A_HALF
            w0c = jnp.clip(w0, 0, length - A_WIN)
            var = jnp.where(w0c > w0, 1, jnp.where(w0c < w0, 2, 0))
            ks = w0c * dil + r - buf0
            if dil == 1:
                q_rows = pl.ds(pl.multiple_of(qs, A_SUB), A_SUB)
                k_rows = pl.ds(pl.multiple_of(ks, A_HALF), A_WIN)
            else:
                q_rows = pl.ds(qs, A_SUB, stride=dil)
                k_rows = pl.ds(ks, A_WIN, stride=dil)
            q = (q_ref[q_rows, :] * scale).astype(BF16)
            k = k_ref[k_rows, :].astype(BF16)
            v = jnp.concatenate([va_ref[k_rows, :], vb_ref[k_rows, :]], axis=1).astype(BF16)
            s = lax.dot_general(q, k, (((1,), (1,)), ((), ())), preferred_element_type=F32)
            s = s + bias_sc[g * 3 + var]
            m = jnp.max(s, axis=-1, keepdims=True)
            p = jnp.exp(s - m)
            l = jnp.sum(p, axis=-1, keepdims=True)
            num = jnp.dot(p.astype(BF16), v, preferred_element_type=F32)
            num_sc[g, 0, q_rows, :] = num[:, :LANES]
            num_sc[g, 1, q_rows, :] = num[:, LANES:]
            m_sc[g, q_rows, :] = jnp.broadcast_to(m, (A_SUB, LANES))
            l_sc[g, q_rows, :] = jnp.broadcast_to(l, (A_SUB, LANES))
            return carry

        lax.fori_loop(0, A_TB // A_SUB, sub_block, 0, unroll=4)

    chunk = 256

    def combine(c, carry):
        rows = pl.ds(pl.multiple_of(c * chunk, chunk), chunk)
        m0, m1, m2 = m_sc[0, rows, :], m_sc[1, rows, :], m_sc[2, rows, :]
        m_max = jnp.maximum(jnp.maximum(m0, m1), m2)
        w0 = jnp.exp(m0 - m_max)
        w1 = jnp.exp(m1 - m_max)
        w2 = jnp.exp(m2 - m_max)
        den = w0 * l_sc[0, rows, :] + w1 * l_sc[1, rows, :] + w2 * l_sc[2, rows, :]
        for half in range(A_V_DIM // LANES):
            num = (w0 * num_sc[0, half, rows, :] + w1 * num_sc[1, half, rows, :]
                   + w2 * num_sc[2, half, rows, :])
            o_ref[rows, half * LANES:(half + 1) * LANES] = (num / den).astype(o_ref.dtype)
        return carry

    lax.fori_loop(0, A_TB // chunk, combine, 0)


def _dilated_attention(qkv):
    b, s, _ = qkv.shape
    win_tokens = A_TB + 2 * A_HALO
    assert s % A_TB == 0 and s >= win_tokens and s // max(d for _, d in A_GROUPS) >= A_WIN
    sq = pl.Squeezed()
    assert A_TB % A_HALO == 0 and (s - win_tokens) % A_HALO == 0
    win_start = lambda tb: jnp.clip(tb * (A_TB // A_HALO) - 1, 0, (s - win_tokens) // A_HALO) * A_HALO

    def q_spec(g):
        return pl.BlockSpec((sq, pl.Element(A_TB), pl.Element(A_QK_DIM)),
                            lambda bi, tb, h: (bi, tb * A_TB, (g * A_HEADS + h) * A_QK_DIM))

    def k_spec(g):
        return pl.BlockSpec((sq, pl.Element(win_tokens), pl.Element(A_QK_DIM)),
                            lambda bi, tb, h: (bi, win_start(tb),
                                               (A_QK_COLS // A_QK_DIM + g * A_HEADS + h) * A_QK_DIM))

    def v_spec(half):
        return pl.BlockSpec((sq, pl.Element(win_tokens), pl.Element(LANES)),
                            lambda bi, tb, h: (bi, win_start(tb),
                                               (2 * A_QK_COLS // LANES + h * (A_V_DIM // LANES) + half) * LANES))

    n_groups = len(A_GROUPS)
    out = pl.pallas_call(
        functools.partial(_dilated_kernel, seq=s),
        grid=(b, s // A_TB, A_HEADS),
        in_specs=[q_spec(0), q_spec(1), q_spec(2), k_spec(0), k_spec(1), k_spec(2), v_spec(0), v_spec(1)],
        out_specs=pl.BlockSpec((None, A_TB, A_V_DIM), lambda bi, tb, h: (bi, tb, h)),
        out_shape=jax.ShapeDtypeStruct((b, s, A_HEADS * A_V_DIM), BF16),
        scratch_shapes=[pltpu.VMEM((3 * n_groups, A_SUB, A_WIN), F32),
                        pltpu.VMEM((n_groups, A_V_DIM // LANES, A_TB, LANES), F32),
                        pltpu.VMEM((n_groups, A_TB, LANES), F32),
                        pltpu.VMEM((n_groups, A_TB, LANES), F32)],
        compiler_params=_params(("arbitrary", "arbitrary", "arbitrary")),
        name="dilated_attn",
    )(qkv, qkv, qkv, qkv, qkv, qkv, qkv, qkv)
    return out.reshape(b * s, A_HEADS * A_V_DIM)


def _diff_key_positions(tk):
    j = np.arange(tk)
    t = np.zeros((tk, LANES), np.float32)
    t[:, 0:3] = (j // B_POS_SPLIT)[:, None]
    t[:, 3:6] = (j % B_POS_SPLIT)[:, None]
    return jnp.asarray(t, dtype=BF16)


def _diff_kernel(q_ref, k_ref, v_ref, kpos_ref, lam_ref, g_ref, o_ref,
                 m_sc, l_sc, acc_sc, alpha_sc, s_sc, p_sc, base_sc, *, t, lambda_init):
    h = pl.program_id(1)
    qi = pl.program_id(2)
    ki = pl.program_id(3)
    rep_k = t // LANES
    rep_v = B_V_DIM // LANES

    @pl.when(ki == 0)
    def _():
        m_sc[...] = jnp.full(m_sc.shape, NEG_INF, F32)
        l_sc[...] = jnp.zeros(l_sc.shape, F32)
        acc_sc[...] = jnp.zeros(acc_sc.shape, F32)

    slope2 = _alibi_slope_vec(h, B_HEADS, (1, LANES)) * LOG2E
    nt = (((1,), (1,)), ((), ()))

    rc = B_ROW_CHUNK
    il = lax.broadcasted_iota(jnp.int32, (rc, LANES), 0).astype(F32)

    def softmax_rows(c, row_term, key_term):
        for i in range(t // rc):
            rows = slice(i * rc, (i + 1) * rc)
            s = s_sc[c, rows, :]
            if key_term is not None:
                s = s - key_term(i)
            a_row = None if row_term is None else row_term(i)
            m_prev = m_sc[c, rows, :]
            m_blk = jnp.max(s, axis=-1, keepdims=True)
            if a_row is not None:
                m_blk = m_blk + a_row
            m_new = jnp.maximum(m_prev, m_blk)
            alpha = jnp.exp2(m_prev - m_new)
            sub = m_new if a_row is None else m_new - a_row
            p = jnp.exp2(s - pltpu.repeat(sub, rep_k, axis=1))
            l_sc[c, rows, :] = alpha * l_sc[c, rows, :] + jnp.sum(p, axis=-1, keepdims=True)
            p_sc[c, rows, :] = p.astype(BF16)
            alpha_sc[c, rows, :] = alpha
            m_sc[c, rows, :] = m_new

    def weighted_values(c):
        acc_sc[c] = (pltpu.repeat(alpha_sc[c], rep_v, axis=1) * acc_sc[c]
                     + jnp.dot(p_sc[c], v_ref[0], preferred_element_type=F32))

    @pl.when(ki != qi)
    def _():
        sigma = jnp.where(qi > ki, 1.0, -1.0).astype(F32)
        delta = ((qi - ki) * t).astype(F32)
        s1 = slope2.astype(BF16).astype(F32)
        r1 = slope2 - s1
        s2 = r1.astype(BF16).astype(F32)
        s3 = r1 - s2
        lane = lax.broadcasted_iota(jnp.int32, (1, LANES), 1)
        piece = jnp.where(lane % 3 == 0, s1, jnp.where(lane % 3 == 1, s2, s3))
        weight = jnp.where(lane < 3, float(B_POS_SPLIT), jnp.where(lane < 6, 1.0, 0.0))
        q_ext = jnp.broadcast_to((piece * weight * sigma).astype(BF16), (t, LANES))
        row_coef = slope2 * (-sigma)
        row_term = lambda i: (il + (delta + float(i * rc))) * row_coef
        k_ext = kpos_ref[...]
        for c in range(2):
            cs = slice(c * B_HEAD_DIM, (c + 1) * B_HEAD_DIM)
            qa = jnp.concatenate([q_ref[0, :, cs], q_ext], axis=1)
            ka = jnp.concatenate([k_ref[0, :, cs], k_ext], axis=1)
            s_sc[c] = lax.dot_general(qa, ka, nt, preferred_element_type=F32)
        for c in range(2):
            softmax_rows(c, row_term, None)
            weighted_values(c)

    @pl.when(ki == qi)
    def _():
        row = lax.broadcasted_iota(jnp.int32, (rc, t), 0)
        col = lax.broadcasted_iota(jnp.int32, (rc, t), 1)
        base_sc[...] = (col - row).astype(F32)
        key_term = lambda i: jnp.abs(base_sc[...] - float(i * rc)) * slope2[:, :1]
        for c in range(2):
            cs = slice(c * B_HEAD_DIM, (c + 1) * B_HEAD_DIM)
            s_sc[c] = lax.dot_general(q_ref[0, :, cs], k_ref[0, :, cs], nt, preferred_element_type=F32)
        for c in range(2):
            softmax_rows(c, None, key_term)
            weighted_values(c)

    @pl.when(ki == pl.num_programs(3) - 1)
    def _():
        lam = (jnp.exp(jnp.sum(lam_ref[0:1, :] * lam_ref[1:2, :], axis=-1, keepdims=True))
               - jnp.exp(jnp.sum(lam_ref[2:3, :] * lam_ref[3:4, :], axis=-1, keepdims=True))
               + lambda_init)
        o0 = acc_sc[0] / pltpu.repeat(l_sc[0], rep_v, axis=1)
        o1 = acc_sc[1] / pltpu.repeat(l_sc[1], rep_v, axis=1)
        o = o0 - lam * o1
        o = o * lax.rsqrt(jnp.mean(o * o, axis=-1, keepdims=True) + SUBLN_EPS)
        o = o * g_ref[...] * (1.0 - lambda_init)
        o_ref[0] = o.astype(o_ref.dtype)


def _diff_attention(qkv, lam_rows, subln_g, lambda_init, t=1024):
    b, s, _ = qkv.shape
    w = 2 * B_HEAD_DIM
    out = pl.pallas_call(
        functools.partial(_diff_kernel, t=t, lambda_init=lambda_init),
        grid=(b, B_HEADS, s // t, s // t),
        in_specs=[
            pl.BlockSpec((1, t, w), lambda bi, h, qi, ki: (bi, qi, h)),
            pl.BlockSpec((1, t, w), lambda bi, h, qi, ki: (bi, ki, B_HEADS + h)),
            pl.BlockSpec((1, t, w), lambda bi, h, qi, ki: (bi, ki, 2 * B_HEADS + h)),
            pl.BlockSpec((t, LANES), lambda bi, h, qi, ki: (0, 0)),
            pl.BlockSpec((4, B_HEAD_DIM), lambda bi, h, qi, ki: (0, 0)),
            pl.BlockSpec((1, B_V_DIM), lambda bi, h, qi, ki: (0, 0)),
        ],
        out_specs=pl.BlockSpec((1, t, w), lambda bi, h, qi, ki: (bi, qi, h)),
        out_shape=jax.ShapeDtypeStruct((b, s, D_MODEL), BF16),
        scratch_shapes=[pltpu.VMEM((2, t, LANES), F32),
                        pltpu.VMEM((2, t, LANES), F32),
                        pltpu.VMEM((2, t, B_V_DIM), F32),
                        pltpu.VMEM((2, t, LANES), F32),
                        pltpu.VMEM((2, t, t), F32),
                        pltpu.VMEM((2, t, t), BF16),
                        pltpu.VMEM((B_ROW_CHUNK, t), F32)],
        compiler_params=_params(("arbitrary", "arbitrary", "arbitrary", "arbitrary")),
        name="diff_attn",
    )(qkv, qkv, qkv, _diff_key_positions(t), lam_rows, subln_g.reshape(1, B_V_DIM))
    return out.reshape(b * s, D_MODEL)


def _na_bias_table(rpb):
    rp = rpb.astype(F32)
    shift = GRID_W - NA_KW
    padded = jnp.pad(rp, ((0, 0), (0, 0), (shift, shift)))
    band = jnp.stack([lax.slice_in_dim(padded, GRID_W - 1 - qc, 2 * GRID_W - 1 - qc, axis=2)
                      for qc in range(GRID_W)], axis=2)
    col = np.arange(GRID_W)
    win_c = np.clip(col - NA_KW // 2, 0, GRID_W - NA_KW)
    col_ok = (col[None, :] >= win_c[:, None]) & (col[None, :] < win_c[:, None] + NA_KW)
    band = jnp.where(jnp.asarray(col_ok)[None, None], band, NEG_INF)
    rows = []
    for qr in range(NA_QROWS):
        dr_idx = [int(np.clip(kr - NA_QROWS - qr + NA_KH - 1, 0, 2 * NA_KH - 2)) for kr in range(NA_KROWS)]
        blk = jnp.stack([band[:, a] for a in dr_idx], axis=2)
        rows.append(blk.reshape(rp.shape[0], GRID_W, NA_KROWS * GRID_W))
    return jnp.concatenate(rows, axis=1)


def _na_row_mask(rows):
    qr = np.arange(NA_QROWS)
    kr = np.arange(NA_KROWS)
    masks = []
    for r0 in (0, NA_QROWS, rows - NA_QROWS):
        win = np.clip(r0 + qr - NA_KH // 2, 0, rows - NA_KH)
        key_row = r0 - NA_QROWS + kr
        ok = (key_row[None, :] >= win[:, None]) & (key_row[None, :] < win[:, None] + NA_KH)
        ok = np.broadcast_to(ok[:, None, :, None], (NA_QROWS, GRID_W, NA_KROWS, GRID_W))
        masks.append(np.where(ok, 0.0, NEG_INF).reshape(NA_QROWS * GRID_W, NA_KROWS * GRID_W))
    return jnp.asarray(np.stack(masks), dtype=F32)


def _na_kernel(q_ref, kp_ref, kc_ref, kn_ref, vp_ref, vc_ref, vn_ref, bias_ref, mask_ref, o_ref, *, hb):
    scale = C_HEAD_DIM ** -0.5
    mask = mask_ref[0]
    for hh in range(hb):
        cs = slice(hh * C_HEAD_DIM, (hh + 1) * C_HEAD_DIM)
        k = jnp.concatenate([kp_ref[0, :, cs], kc_ref[0, :, cs], kn_ref[0, :, cs]], axis=0)
        v = jnp.concatenate([vp_ref[0, :, cs], vc_ref[0, :, cs], vn_ref[0, :, cs]], axis=0)
        s = lax.dot_general(q_ref[0, :, cs], k, (((1,), (1,)), ((), ())), preferred_element_type=F32)
        s = s * scale + bias_ref[hh] + mask
        m = jnp.max(s, axis=-1, keepdims=True)
        p = jnp.exp(s - m)
        l = jnp.sum(p, axis=-1, keepdims=True)
        o = jnp.dot(p.astype(BF16), v, preferred_element_type=F32) / l
        o_ref[0, :, cs] = o.astype(o_ref.dtype)


def _na_attention(qkv, rpb, hb=4):
    b, s, _ = qkv.shape
    rows = s // GRID_W
    tq = NA_QROWS * GRID_W
    nblk = s // tq
    w = hb * C_HEAD_DIM
    hgroups = C_HEADS // hb
    bias = _na_bias_table(rpb)
    mask = _na_row_mask(rows)
    prev = lambda r: jnp.maximum(r - 1, 0)
    nxt = lambda r: jnp.minimum(r + 1, nblk - 1)
    variant = lambda r: jnp.where(r == 0, 0, jnp.where(r == nblk - 1, 2, 1))
    kcol = lambda hg: hgroups + hg
    vcol = lambda hg: 2 * hgroups + hg
    out = pl.pallas_call(
        functools.partial(_na_kernel, hb=hb),
        grid=(b, hgroups, nblk),
        in_specs=[
            pl.BlockSpec((1, tq, w), lambda bi, hg, r: (bi, r, hg)),
            pl.BlockSpec((1, tq, w), lambda bi, hg, r: (bi, prev(r), kcol(hg))),
            pl.BlockSpec((1, tq, w), lambda bi, hg, r: (bi, r, kcol(hg))),
            pl.BlockSpec((1, tq, w), lambda bi, hg, r: (bi, nxt(r), kcol(hg))),
            pl.BlockSpec((1, tq, w), lambda bi, hg, r: (bi, prev(r), vcol(hg))),
            pl.BlockSpec((1, tq, w), lambda bi, hg, r: (bi, r, vcol(hg))),
            pl.BlockSpec((1, tq, w), lambda bi, hg, r: (bi, nxt(r), vcol(hg))),
            pl.BlockSpec((hb, tq, NA_KROWS * GRID_W), lambda bi, hg, r: (hg, 0, 0)),
            pl.BlockSpec((1, tq, NA_KROWS * GRID_W), lambda bi, hg, r: (variant(r), 0, 0)),
        ],
        out_specs=pl.BlockSpec((1, tq, w), lambda bi, hg, r: (bi, r, hg)),
        out_shape=jax.ShapeDtypeStruct((b, s, D_MODEL), BF16),
        compiler_params=_params(("arbitrary", "arbitrary", "arbitrary")),
        name="na_attn",
    )(qkv, qkv, qkv, qkv, qkv, qkv, qkv, bias, mask)
    return out.reshape(b * s, D_MODEL)


def _lambda_init(layer):
    return 0.8 - 0.6 * math.exp(-0.3 * layer)


def kernel(x, mix_norm_g, mlp_norm_g, final_norm_g, a_w_in, a_w_out, b_w_in, b_w_out,
           b_lambda_q1, b_lambda_k1, b_lambda_q2, b_lambda_k2, b_subln_g,
           c_w_in, c_w_out, c_rpb, mlp_w_up, mlp_w_down):
    b, s, d = x.shape
    depth = mix_norm_g.shape[0]
    xr = x.reshape(b * s, d)
    for i in range(depth):
        kind, j = i % 3, i // 3
        h = _rmsnorm(xr, mix_norm_g[i], BF16)
        if kind == 0:
            qkv = _matmul(h, a_w_in, j, epilogue="cast", out_dtype=F32)
            o = _dilated_attention(qkv.reshape(b, s, A_IN))
            xr = _matmul(o, a_w_out, j, epilogue="residual", res=xr)
        elif kind == 1:
            qkv = _matmul(h, b_w_in, j, epilogue="cast", scaled_cols=d, scale=B_HEAD_DIM ** -0.5 * LOG2E)
            lam_rows = jnp.stack([b_lambda_q1[j], b_lambda_k1[j], b_lambda_q2[j], b_lambda_k2[j]]).astype(F32)
            o = _diff_attention(qkv.reshape(b, s, 3 * d), lam_rows, b_subln_g[j].astype(F32), _lambda_init(i))
            xr = _matmul(o, b_w_out, j, epilogue="residual", res=xr)
        else:
            qkv = _matmul(h, c_w_in, j, epilogue="cast")
            o = _na_attention(qkv.reshape(b, s, 3 * d), c_rpb[j])
            xr = _matmul(o, c_w_out, j, epilogue="residual", res=xr)
        h = _rmsnorm(xr, mlp_norm_g[i], BF16)
        u = _matmul(h, mlp_w_up, i, epilogue="relu2")
        for kb in range(mlp_w_up.shape[2] // K_SLICE):
            xr = _matmul(u, mlp_w_down, i, epilogue="residual", k_block=kb, res=xr)
    return _rmsnorm(xr, final_norm_g, F32).reshape(b, s, d)
```
